```python
import math
import jax, jax.numpy as jnp
from jax import lax
import numpy as np

D_MODEL = 2048
BATCH = 1
SEQ = 8192
DEPTH = 1
DEC_BATCH = 8
DEC_SEQ = 64
PAST_LEN = 1024

CHUNK = 64
EPS = 1e-6
GLA_HEADS = 4
GLA_DK = 128
GLA_DV = 256
GLA_LOWRANK = 16
GLA_GATE_NORM = 16.0
SB_HEADS = 8
SB_HD = 128
SB_BLOCK = 128
PEER_HEADS = 8
PEER_NKEYS = 128
PEER_NEXPERTS = PEER_NKEYS * PEER_NKEYS
PEER_DQ = 256
PEER_TOPK = 16
PEER_TOKEN_BLOCK = 128

GLA_QK = GLA_HEADS * GLA_DK
GLA_V = GLA_HEADS * GLA_DV
SB_W = SB_HEADS * SB_HD
MIX_W = GLA_V + SB_W
IN_SIZES = [GLA_QK, GLA_QK, GLA_V, GLA_V, GLA_LOWRANK, SB_W, SB_W, SB_W]
IN_COLS = sum(IN_SIZES)

kernel_name = 'hymba_gla_stickbreaking_peer_stream_step'

F32 = jnp.float32


def _rmsnorm(x, g):
    xf = x.astype(F32)
    y = xf * lax.rsqrt(jnp.mean(xf * xf, axis=-1, keepdims=True) + EPS) * g.astype(F32)
    return y.astype(x.dtype)


def _project(xn, w_in, w_gk2, b_gk):
    B, T, _ = xn.shape
    p = jnp.einsum('btd,dc->btc', xn, w_in)
    idx = [int(s) for s in np.cumsum(IN_SIZES)[:-1]]
    qa, ka, va, ga, la, qb, kb, vb = jnp.split(p, idx, axis=-1)
    qa = qa.astype(F32).reshape(B, T, GLA_HEADS, GLA_DK) * (GLA_DK ** -0.5)
    ka = ka.astype(F32).reshape(B, T, GLA_HEADS, GLA_DK)
    va = va.astype(F32).reshape(B, T, GLA_HEADS, GLA_DV)
    gk = jax.nn.log_sigmoid((jnp.einsum('btr,rc->btc', la, w_gk2) + b_gk).astype(F32)) / GLA_GATE_NORM
    gk = gk.reshape(B, T, GLA_HEADS, GLA_DK)
    qb = qb.astype(F32).reshape(B, T, SB_HEADS, SB_HD)
    kb = kb.reshape(B, T, SB_HEADS, SB_HD)
    vb = vb.reshape(B, T, SB_HEADS, SB_HD)
    return qa, ka, va, ga, gk, qb, kb, vb


def _gla_chunk(S, q, k, v, gk):
    C = q.shape[1]
    b = jnp.cumsum(gk, axis=1)
    o_inter = jnp.einsum('bthk,bhkv->bthv', q * jnp.exp(b), S)
    causal = jnp.arange(C)[:, None] >= jnp.arange(C)[None, :]
    rel = b[:, :, None] - b[:, None, :]
    decay = jnp.exp(jnp.where(causal[None, :, :, None, None], rel, -jnp.inf))
    scores = jnp.einsum('bthk,bshk,btshk->bhts', q, k, decay)
    o_intra = jnp.einsum('bhts,bshv->bthv', scores, v)
    b_last = b[:, -1]
    k_dec = k * jnp.exp(b_last[:, None] - b)
    S_new = jnp.exp(b_last)[..., None] * S + jnp.einsum('bshk,bshv->bhkv', k_dec, v)
    return S_new, o_inter + o_intra


def _gla_sequence(S0, q, k, v, gk):
    B, T = q.shape[0], q.shape[1]
    n = T // CHUNK

    def to_chunks(a):
        return jnp.moveaxis(a.reshape(B, n, CHUNK, *a.shape[2:]), 1, 0)

    S, o = lax.scan(lambda s, inp: _gla_chunk(s, *inp), S0,
                    (to_chunks(q), to_chunks(k), to_chunks(v), to_chunks(gk)))
    return S, jnp.moveaxis(o, 0, 1).reshape(B, T, GLA_HEADS, GLA_DV)


def _sb_block(q, k, v, q_pos, k_pos):
    z = jnp.einsum('bqhd,bkhd->bhqk', q, k) / math.sqrt(SB_HD)
    mask = (k_pos[None, :] < q_pos[:, None])[None, None]
    log_beta = jax.nn.log_sigmoid(z)
    log_1mb = jnp.where(mask, jax.nn.log_sigmoid(-z), 0.0)
    tail = lax.cumsum(log_1mb, axis=3, reverse=True) - log_1mb
    a = jnp.where(mask, jnp.exp(log_beta + tail), 0.0)
    return jnp.einsum('bhqk,bkhd->bqhd', a, v)


def _sb_sequence(q, k, v):
    B, T = q.shape[0], q.shape[1]
    n = T // SB_BLOCK
    pos = jnp.arange(T)
    qblocks = jnp.moveaxis(q.reshape(B, n, SB_BLOCK, SB_HEADS, SB_HD), 1, 0)
    pblocks = pos.reshape(n, SB_BLOCK)
    o = lax.map(lambda args: _sb_block(args[0], k, v, args[1], pos), (qblocks, pblocks))
    return jnp.moveaxis(o, 0, 1).reshape(B, T, SB_HEADS, SB_HD)


def _peer_tokens(xn, w_query, sub_keys, expert_u, expert_v):
    T = xn.shape[0]
    q = jnp.einsum('td,dc->tc', xn, w_query).astype(F32).reshape(T, PEER_HEADS, 2, PEER_DQ // 2)
    s = jnp.einsum('thpc,phnc->thpn', q, sub_keys.astype(F32))
    s_top, i_top = lax.top_k(s, PEER_TOPK)
    cand = s_top[:, :, 0, :, None] + s_top[:, :, 1, None, :]
    cand_idx = i_top[:, :, 0, :, None] * PEER_NKEYS + i_top[:, :, 1, None, :]
    cand = cand.reshape(T, PEER_HEADS, PEER_TOPK * PEER_TOPK)
    cand_idx = cand_idx.reshape(T, PEER_HEADS, PEER_TOPK * PEER_TOPK)
    best, sel = lax.top_k(cand, PEER_TOPK)
    idx = jnp.take_along_axis(cand_idx, sel, axis=-1)
    g = jax.nn.softmax(best, axis=-1)
    u = expert_u[idx]
    act = jax.nn.gelu(jnp.einsum('td,thkd->thk', xn, u).astype(F32), approximate=False)
    w = (g * act).astype(xn.dtype)
    return jnp.einsum('thk,thkd->td', w, expert_v[idx])


def _peer(xn, w_query, sub_keys, expert_u, expert_v):
    B, T, D = xn.shape
    n = B * T
    flat = xn.reshape(n, D)
    if n % PEER_TOKEN_BLOCK == 0 and n > PEER_TOKEN_BLOCK:
        out = lax.map(lambda xb: _peer_tokens(xb, w_query, sub_keys, expert_u, expert_v),
                      flat.reshape(n // PEER_TOKEN_BLOCK, PEER_TOKEN_BLOCK, D)).reshape(n, D)
    else:
        out = _peer_tokens(flat, w_query, sub_keys, expert_u, expert_v)
    return out.reshape(B, T, D)


def _finish(x, o_gla, ga, o_sb, gla_norm_g, sb_norm_g, w_out, norm2_g, w_query, sub_keys,
            expert_u, expert_v, final_g):
    B, T, _ = x.shape
    a = _rmsnorm(o_gla, gla_norm_g).reshape(B, T, GLA_V) * jax.nn.silu(ga.astype(F32))
    s = _rmsnorm(o_sb, sb_norm_g).reshape(B, T, SB_W)
    mixed = jnp.concatenate([a, s], axis=-1).astype(x.dtype)
    h = x + jnp.einsum('btc,cd->btd', mixed, w_out)
    h = h + _peer(_rmsnorm(h, norm2_g), w_query, sub_keys, expert_u, expert_v)
    return _rmsnorm(h, final_g)


def setup_inputs(seed: int = 0) -> dict:
    key = jax.random.key(seed)
    ks = jax.random.split(key, 20)
    nrm = jax.random.normal
    return {
        'x_prompt': nrm(ks[0], (BATCH, SEQ, D_MODEL), F32),
        'x_sample': nrm(ks[1], (DEC_BATCH, DEC_SEQ, D_MODEL), F32),
        'cache_k': nrm(ks[2], (DEC_BATCH, PAST_LEN, SB_HEADS, SB_HD), F32),
        'cache_v': nrm(ks[3], (DEC_BATCH, PAST_LEN, SB_HEADS, SB_HD), F32),
        'state_gla': 0.5 * nrm(ks[4], (DEC_BATCH, GLA_HEADS, GLA_DK, GLA_DV), F32),
        'norm1_g': 1.0 + 0.02 * nrm(ks[5], (D_MODEL,), F32),
        'w_in': nrm(ks[6], (D_MODEL, IN_COLS), F32) * D_MODEL ** -0.5,
        'w_gk2': nrm(ks[7], (GLA_LOWRANK, GLA_QK), F32) * GLA_LOWRANK ** -0.5,
        'b_gk': 0.01 * nrm(ks[8], (GLA_QK,), F32),
        'gla_norm_g': 1.0 + 0.02 * nrm(ks[9], (GLA_DV,), F32),
        'sb_norm_g': 1.0 + 0.02 * nrm(ks[10], (SB_HD,), F32),
        'w_out': nrm(ks[11], (MIX_W, D_MODEL), F32) * MIX_W ** -0.5,
        'norm2_g': 1.0 + 0.02 * nrm(ks[12], (D_MODEL,), F32),
        'w_query': nrm(ks[13], (D_MODEL, PEER_HEADS * PEER_DQ), F32) * D_MODEL ** -0.5,
        'sub_keys': nrm(ks[14], (2, PEER_HEADS, PEER_NKEYS, PEER_DQ // 2), F32) * (PEER_DQ // 2) ** -0.5,
        'expert_u': nrm(ks[15], (PEER_NEXPERTS, D_MODEL), F32) * D_MODEL ** -0.5,
        'expert_v': nrm(ks[16], (PEER_NEXPERTS, D_MODEL), F32) * PEER_HEADS ** -0.5,
        'final_g': 1.0 + 0.02 * nrm(ks[17], (D_MODEL,), F32),
    }


def reference(x_prompt, x_sample, cache_k, cache_v, state_gla, norm1_g, w_in, w_gk2, b_gk,
              gla_norm_g, sb_norm_g, w_out, norm2_g, w_query, sub_keys, expert_u, expert_v,
              final_g):
    xn = _rmsnorm(x_prompt, norm1_g)
    qa, ka, va, ga, gk, qb, kb_p, vb_p = _project(xn, w_in, w_gk2, b_gk)
    s0 = jnp.zeros((x_prompt.shape[0], GLA_HEADS, GLA_DK, GLA_DV), F32)
    state_gla_p, o_gla = _gla_sequence(s0, qa, ka, va, gk)
    o_sb = _sb_sequence(qb, kb_p.astype(F32), vb_p.astype(F32))
    y_prompt = _finish(x_prompt, o_gla, ga, o_sb, gla_norm_g, sb_norm_g, w_out, norm2_g,
                       w_query, sub_keys, expert_u, expert_v, final_g)

    xs = _rmsnorm(x_sample, norm1_g)
    qa_s, ka_s, va_s, ga_s, gk_s, qb_s, kb_s, vb_s = _project(xs, w_in, w_gk2, b_gk)
    state_gla_s, o_gla_s = _gla_chunk(state_gla.astype(F32), qa_s, ka_s, va_s, gk_s)
    t_new = x_sample.shape[1]
    k_all = jnp.concatenate([cache_k.astype(F32), kb_s.astype(F32)], axis=1)
    v_all = jnp.concatenate([cache_v.astype(F32), vb_s.astype(F32)], axis=1)
    k_pos = jnp.arange(PAST_LEN + t_new)
    q_pos = PAST_LEN + jnp.arange(t_new)
    o_sb_s = _sb_block(qb_s, k_all, v_all, q_pos, k_pos)
    y_sample = _finish(x_sample, o_gla_s, ga_s, o_sb_s, gla_norm_g, sb_norm_g, w_out, norm2_g,
                       w_query, sub_keys, expert_u, expert_v, final_g)

    return (y_prompt, y_sample, state_gla_p, kb_p, vb_p, state_gla_s, kb_s, vb_s)
```

```python
import functools
import math

import jax
import jax.numpy as jnp
from jax import lax
from jax.experimental import pallas as pl
from jax.experimental.pallas import tpu as pltpu

F32 = jnp.float32
BF16 = jnp.bfloat16

EPS = 1e-6
CHUNK = 64
SUB = 16
GLA_HEADS = 4
GLA_DK = 128
GLA_DV = 256
GLA_LOWRANK = 16
GLA_GATE_NORM = 16.0
SB_HEADS = 8
SB_HD = 128
PEER_HEADS = 8
PEER_NKEYS = 128
PEER_DQ = 256
PEER_TOPK = 16

GLA_QK = GLA_HEADS * GLA_DK
GLA_V = GLA_HEADS * GLA_DV
SB_W = SB_HEADS * SB_HD

LANES = 128
EXP_ZERO_BELOW = -105.0
VMEM_LIMIT = 56 * 1024 * 1024

_HIGHEST = lax.Precision.HIGHEST


def _dot(a, b, precision=None):
    return lax.dot_general(a, b, (((1,), (0,)), ((), ())), precision=precision,
                           preferred_element_type=F32)


def _dot_nt(a, b, precision=None):
    return lax.dot_general(a, b, (((1,), (1,)), ((), ())), precision=precision,
                           preferred_element_type=F32)


def _softplus_neg_abs(z):
    return jnp.log1p(jnp.exp(-jnp.abs(z)))


def _resident(shape):
    nd = len(shape)
    return pl.BlockSpec(shape, lambda *_: (0,) * nd, pipeline_mode=pl.Buffered(1))


def _params(sem):
    return pltpu.CompilerParams(dimension_semantics=sem, vmem_limit_bytes=VMEM_LIMIT)


def _in_proj_kernel(x_ref, g_ref, w_ref, wla_ref, wgk_ref, bgk_ref,
                    qa_ref, ka_ref, va_ref, ga_ref, gk_ref, qb_ref, kb_ref, vb_ref, kb16_ref, vb16_ref):
    x = x_ref[...]
    xn = x * lax.rsqrt(jnp.mean(x * x, axis=-1, keepdims=True) + EPS) * g_ref[...]
    xb = xn.astype(BF16)

    def proj(lo, width):
        return _dot(xb, w_ref[:, lo:lo + width])

    c = 0
    qa_ref[...] = proj(c, GLA_QK) * (GLA_DK ** -0.5); c += GLA_QK
    ka_ref[...] = proj(c, GLA_QK); c += GLA_QK
    va_ref[...] = proj(c, GLA_V); c += GLA_V
    ga_ref[...] = proj(c, GLA_V); c += GLA_V
    qb_ref[...] = (proj(c, SB_W) * (1.0 / math.sqrt(SB_HD))).astype(BF16); c += SB_W
    kb = proj(c, SB_W); c += SB_W
    vb = proj(c, SB_W)
    kb_ref[...] = kb
    vb_ref[...] = vb
    kb16_ref[...] = kb.astype(BF16)
    vb16_ref[...] = vb.astype(BF16)
    la = _dot(xn, wla_ref[...], _HIGHEST)
    pre = _dot(la, wgk_ref[...], _HIGHEST) + bgk_ref[...]
    gk_ref[...] = (jnp.minimum(pre, 0.0) - _softplus_neg_abs(pre)) * (1.0 / GLA_GATE_NORM)


def _in_proj(x, norm1_g, w_main, w_la, w_gk2p, b_gk, tb):
    t, d = x.shape
    nb = t // tb
    row = lambda w: pl.BlockSpec((tb, w), lambda i: (i, 0))
    outs = [(GLA_QK, F32), (GLA_QK, F32), (GLA_V, F32), (GLA_V, F32), (GLA_QK, F32),
            (SB_W, BF16), (SB_W, F32), (SB_W, F32), (SB_W, BF16), (SB_W, BF16)]
    return pl.pallas_call(
        _in_proj_kernel,
        grid=(nb,),
        in_specs=[row(d), _resident((1, d)), _resident(w_main.shape), _resident(w_la.shape),
                  _resident(w_gk2p.shape), _resident((1, GLA_QK))],
        out_specs=[row(w) for w, _ in outs],
        out_shape=[jax.ShapeDtypeStruct((t, w), dt) for w, dt in outs],
        compiler_params=_params(("parallel",)),
        name="in_proj",
    )(x, norm1_g, w_main, w_la, w_gk2p, b_gk)


def _gla_chunk(q, k, v, gk, st):
    c = CHUNK
    nsub = c // SUB
    row = lax.broadcasted_iota(jnp.int32, (c, c), 0)
    col = lax.broadcasted_iota(jnp.int32, (c, c), 1)
    tril = (row >= col).astype(F32)
    b = _dot(tril, gk, _HIGHEST)
    b_last = b[c - 1:c, :]
    o = _dot_nt((q * jnp.exp(b)).astype(BF16), st.astype(BF16))

    b3 = b.reshape(nsub, SUB, GLA_DK)
    q3 = q.reshape(nsub, SUB, GLA_DK)
    k3 = k.reshape(nsub, SUB, GLA_DK)
    blk = lax.broadcasted_iota(jnp.int32, (nsub, SUB, c), 0)
    trow = lax.broadcasted_iota(jnp.int32, (nsub, SUB, c), 1)
    lane = lax.broadcasted_iota(jnp.int32, (nsub, SUB, c), 2)
    key_in_blk = lane - SUB * blk
    diag = jnp.zeros((nsub, SUB, c), F32)
    for s in range(SUB):
        dec = jnp.exp(jnp.minimum(b3 - b3[:, s:s + 1, :], 0.0))
        colsum = jnp.sum(q3 * dec * k3[:, s:s + 1, :], axis=-1, keepdims=True)
        diag = jnp.where(key_in_blk == s, jnp.where(trow >= s, colsum, 0.0), diag)
    diag = diag.reshape(c, c)

    lane2 = lax.broadcasted_iota(jnp.int32, (SUB, c), 1)
    rows = [diag[0:SUB, :]]
    for i in range(1, nsub):
        ref_b = b[i * SUB:i * SUB + 1, :]
        qi = q[i * SUB:(i + 1) * SUB, :] * jnp.exp(b[i * SUB:(i + 1) * SUB, :] - ref_b)
        ki = k * jnp.exp(jnp.minimum(ref_b - b, 0.0))
        p = _dot_nt(qi.astype(BF16), ki.astype(BF16))
        rows.append(jnp.where(lane2 < i * SUB, p, diag[i * SUB:(i + 1) * SUB, :]))
    scores = jnp.concatenate(rows, axis=0)
    vb = v.astype(BF16)
    o = o + _dot(scores.astype(BF16), vb)

    kd = (k * jnp.exp(b_last - b)).astype(BF16)
    st_new = st * jnp.exp(b_last) + _dot(v.T.astype(BF16), kd)
    return o, st_new


def _gla_kernel(q_ref, k_ref, v_ref, gk_ref, ga_ref, gn_ref, s0_ref, a_ref, sfin_ref, st_ref, *, nchunk):
    g = pl.program_id(2)

    @pl.when(g == 0)
    def _():
        st_ref[...] = s0_ref[0, 0].T

    st = st_ref[...]
    for c in range(nchunk):
        r = slice(c * CHUNK, (c + 1) * CHUNK)
        o, st = _gla_chunk(q_ref[r, :], k_ref[r, :], v_ref[r, :], gk_ref[r, :], st)
        on = o * lax.rsqrt(jnp.mean(o * o, axis=-1, keepdims=True) + EPS) * gn_ref[...]
        ga = ga_ref[r, :]
        a_ref[r, :] = (on * (ga / (1.0 + jnp.exp(-ga)))).astype(BF16)
    st_ref[...] = st

    @pl.when(g == pl.num_programs(2) - 1)
    def _():
        sfin_ref[0, 0] = st.T


def _gla(qa, ka, va, gk, ga, gla_norm_g, s0, nchunk):
    t = qa.shape[0]
    nseq = s0.shape[0]
    tg = nchunk * CHUNK
    ng = t // nseq // tg
    tok = lambda w: pl.BlockSpec((tg, w), lambda h, s, g: (s * ng + g, h))
    state = pl.BlockSpec((1, 1, GLA_DK, GLA_DV), lambda h, s, g: (s, h, 0, 0))
    return pl.pallas_call(
        functools.partial(_gla_kernel, nchunk=nchunk),
        grid=(GLA_HEADS, nseq, ng),
        in_specs=[tok(GLA_DK), tok(GLA_DK), tok(GLA_DV), tok(GLA_DK), tok(GLA_DV),
                  pl.BlockSpec((1, GLA_DV), lambda h, s, g: (0, 0)), state],
        out_specs=[tok(GLA_DV), state],
        out_shape=[jax.ShapeDtypeStruct((t, GLA_V), BF16),
                   jax.ShapeDtypeStruct((nseq, GLA_HEADS, GLA_DK, GLA_DV), F32)],
        scratch_shapes=[pltpu.VMEM((GLA_DV, GLA_DK), F32)],
        compiler_params=_params(("parallel", "arbitrary", "arbitrary")),
        name="gla",
    )(qa, ka, va, gk, ga, gla_norm_g, s0)


def _sb_kernel(q_ref, k_ref, v_ref, gn_ref, o_ref, acc_ref, carry_ref, *, blk, first_blk):
    i = pl.program_id(2)
    q = q_ref[...]
    row = lax.broadcasted_iota(jnp.int32, (blk, blk), 0)
    col = lax.broadcasted_iota(jnp.int32, (blk, blk), 1)
    later = (row > col).astype(BF16)
    visible = col < row

    def step(kb, masked):
        start = pl.multiple_of(kb * blk, blk)
        z = _dot_nt(q, k_ref[pl.ds(start, blk), :])
        sp = _softplus_neg_abs(z)
        log_beta = jnp.minimum(z, 0.0) - sp
        log_1mb = -(jnp.maximum(z, 0.0) + sp)
        if masked:
            log_1mb = jnp.where(visible, log_1mb, 0.0)
        hi = log_1mb.astype(BF16)
        lo = (log_1mb - hi.astype(F32)).astype(BF16)
        tail = _dot(hi, later) + _dot(lo, later)
        carry = carry_ref[...]
        a = jnp.exp(log_beta + tail + carry)
        if masked:
            a = jnp.where(visible, a, 0.0)
        acc_ref[...] += _dot(a.astype(BF16), v_ref[pl.ds(start, blk), :])
        carry = carry + tail[:, 0:1] + log_1mb[:, 0:1]
        carry_ref[...] = carry
        return jnp.max(carry)

    acc_ref[...] = jnp.zeros_like(acc_ref)
    carry_ref[...] = jnp.zeros_like(carry_ref)
    top = step(first_blk + i, True)

    def cond(state):
        kb, mx = state
        return jnp.logical_and(kb >= 0, mx > EXP_ZERO_BELOW)

    def body(state):
        kb, _ = state
        return kb - 1, step(kb, False)

    lax.while_loop(cond, body, (first_blk + i - 1, top))

    o = acc_ref[...]
    on = o * lax.rsqrt(jnp.mean(o * o, axis=-1, keepdims=True) + EPS) * gn_ref[...]
    o_ref[...] = on.astype(BF16)


def _sb(q16, k16, v16, sb_norm_g, nseq, blk):
    tq = q16.shape[0] // nseq
    tk = k16.shape[0] // nseq
    nq = tq // blk
    first_blk = (tk - tq) // blk
    qspec = pl.BlockSpec((blk, SB_HD), lambda s, h, i: (s * nq + i, h))
    kspec = pl.BlockSpec((tk, SB_HD), lambda s, h, i: (s, h))
    return pl.pallas_call(
        functools.partial(_sb_kernel, blk=blk, first_blk=first_blk),
        grid=(nseq, SB_HEADS, nq),
        in_specs=[qspec, kspec, kspec, pl.BlockSpec((1, SB_HD), lambda s, h, i: (0, 0))],
        out_specs=qspec,
        out_shape=jax.ShapeDtypeStruct(q16.shape, BF16),
        scratch_shapes=[pltpu.VMEM((blk, SB_HD), F32), pltpu.VMEM((blk, 1), F32)],
        compiler_params=_params(("parallel", "parallel", "arbitrary")),
        name="sb",
    )(q16, k16, v16, sb_norm_g)


def _mid_kernel(x_ref, a_ref, s_ref, wo_ref, g2_ref, wq_ref, sk_ref, h_ref, hn_ref, st_ref):
    h = x_ref[...] + _dot(a_ref[...], wo_ref[0:GLA_V, :]) + _dot(s_ref[...], wo_ref[GLA_V:GLA_V + SB_W, :])
    h_ref[...] = h
    hn = (h * lax.rsqrt(jnp.mean(h * h, axis=-1, keepdims=True) + EPS) * g2_ref[...]).astype(BF16)
    hn_ref[...] = hn
    q = _dot(hn, wq_ref[...])
    half = PEER_DQ // 2
    for hp in range(PEER_HEADS * 2):
        head, part = divmod(hp, 2)
        qs = q[:, hp * half:(hp + 1) * half].astype(BF16)
        st_ref[hp * PEER_NKEYS:(hp + 1) * PEER_NKEYS, :] = _dot_nt(sk_ref[part, head], qs)


def _mid(x, a16, s16, w_out16, norm2_g, w_query16, sub_keys16, tb):
    t, d = x.shape
    nb = t // tb
    row = lambda w: pl.BlockSpec((tb, w), lambda i: (i, 0))
    nrow = PEER_HEADS * 2 * PEER_NKEYS
    return pl.pallas_call(
        _mid_kernel,
        grid=(nb,),
        in_specs=[row(d), row(GLA_V), row(SB_W), _resident(w_out16.shape), _resident((1, d)),
                  _resident(w_query16.shape), _resident(sub_keys16.shape)],
        out_specs=[row(d), row(d), pl.BlockSpec((nrow, tb), lambda i: (0, i))],
        out_shape=[jax.ShapeDtypeStruct((t, d), F32), jax.ShapeDtypeStruct((t, d), BF16),
                   jax.ShapeDtypeStruct((nrow, t), F32)],
        compiler_params=_params(("parallel",)),
        name="mid",
    )(x, a16, s16, w_out16, norm2_g, w_query16, sub_keys16)


NOT_SELECTED = 99.0


def _top_rows(s, k):
    n = s.shape[0]
    idx = lax.broadcasted_iota(jnp.int32, s.shape, 0).astype(F32)
    rank = jnp.full(s.shape, NOT_SELECTED, F32)
    work = s
    vals = []
    for r in range(k):
        m = jnp.max(work, axis=0, keepdims=True)
        first = jnp.min(jnp.where(work == m, idx, float(n)), axis=0, keepdims=True)
        hit = idx == first
        rank = jnp.where(hit, float(r), rank)
        work = jnp.where(hit, -jnp.inf, work)
        vals.append(m)
    return jnp.concatenate(vals, axis=0), rank


def _route_kernel(st_ref, e0_ref, n_ref, e1_ref, r_ref):
    k = PEER_TOPK
    nk = PEER_NKEYS
    for head in range(PEER_HEADS):
        s0 = st_ref[(2 * head) * nk:(2 * head + 1) * nk, :]
        s1 = st_ref[(2 * head + 1) * nk:(2 * head + 2) * nk, :]
        v0, rank0 = _top_rows(s0, k)
        v1, rank1 = _top_rows(s1, k)
        cand = jnp.concatenate([v0[a:a + 1, :] + v1 for a in range(k)], axis=0)
        best, crank = _top_rows(cand, k)
        chosen = (crank < float(k)).astype(F32)
        counts = [jnp.sum(chosen[a * k:(a + 1) * k, :], axis=0, keepdims=True) for a in range(k)]
        z = jnp.sum(jnp.exp(best - best[0:1, :]), axis=0, keepdims=True)
        nsel = jnp.zeros_like(s0)
        for a in range(k):
            nsel = jnp.where(rank0 == float(a), counts[a], nsel)
        rows = slice(head * nk, (head + 1) * nk)
        e0_ref[rows, :] = jnp.exp(s0 - v0[0:1, :]) / z
        n_ref[rows, :] = nsel
        e1_ref[rows, :] = jnp.exp(s1 - v1[0:1, :])
        r_ref[rows, :] = rank1


def _route(scores_t, tl):
    nrow, t = scores_t.shape
    half = nrow // 2
    spec = pl.BlockSpec((half, tl), lambda i: (0, i))
    return pl.pallas_call(
        _route_kernel,
        grid=(t // tl,),
        in_specs=[pl.BlockSpec((nrow, tl), lambda i: (0, i))],
        out_specs=[spec] * 4,
        out_shape=[jax.ShapeDtypeStruct((half, t), F32)] * 4,
        compiler_params=_params(("parallel",)),
        name="route",
    )(scores_t)


def _experts_kernel(hn_ref, u_ref, v_ref, e0_ref, n_ref, e1_ref, r_ref, h_ref, gf_ref, y_ref, acc_ref, pt_ref, *, ec):
    e = pl.program_id(1)
    nk = PEER_NKEYS

    @pl.when(e == 0)
    def _():
        acc_ref[...] = jnp.zeros_like(acc_ref)

    pre = _dot_nt(u_ref[...], hn_ref[...])
    for ii in range(ec // nk):
        key0 = e * (ec // nk) + ii
        w = jnp.zeros((nk, pre.shape[1]), F32)
        for head in range(PEER_HEADS):
            e0 = e0_ref[pl.ds(head * nk + key0, 1), :]
            n0 = n_ref[pl.ds(head * nk + key0, 1), :]
            rows = slice(head * nk, (head + 1) * nk)
            w = w + jnp.where(r_ref[rows, :] < n0, e0 * e1_ref[rows, :], 0.0)
        x = pre[ii * nk:(ii + 1) * nk, :]
        act = 0.5 * x * (1.0 + lax.erf(x * (1.0 / math.sqrt(2.0))))
        pt_ref[ii * nk:(ii + 1) * nk, :] = (w * act).astype(BF16)
    acc_ref[...] += lax.dot_general(pt_ref[...], v_ref[...], (((0,), (0,)), ((), ())),
                                    preferred_element_type=F32)

    @pl.when(e == pl.num_programs(1) - 1)
    def _():
        h2 = h_ref[...] + acc_ref[...]
        y_ref[...] = h2 * lax.rsqrt(jnp.mean(h2 * h2, axis=-1, keepdims=True) + EPS) * gf_ref[...]


def _experts(hn16, u16, v16, e0, nsel, e1, rank1, h, final_g, tb, ec):
    t, d = h.shape
    ne = u16.shape[0]
    tok = pl.BlockSpec((tb, d), lambda i, e: (i, 0))
    wspec = pl.BlockSpec((ec, d), lambda i, e: (e, 0))
    rspec = pl.BlockSpec((e0.shape[0], tb), lambda i, e: (0, i))
    return pl.pallas_call(
        functools.partial(_experts_kernel, ec=ec),
        grid=(t // tb, ne // ec),
        in_specs=[tok, wspec, wspec, rspec, rspec, rspec, rspec, tok,
                  pl.BlockSpec((1, d), lambda i, e: (0, 0))],
        out_specs=tok,
        out_shape=jax.ShapeDtypeStruct((t, d), F32),
        scratch_shapes=[pltpu.VMEM((tb, d), F32), pltpu.VMEM((ec, tb), BF16)],
        compiler_params=_params(("parallel", "arbitrary")),
        name="experts",
    )(hn16, u16, v16, e0, nsel, e1, rank1, h, final_g)


def _block_rows(t, want):
    b = min(want, t)
    while t % b:
        b //= 2
    return b


def _group(x, state0, k_past16, v_past16, w, gla_chunks, sb_blk):
    nseq, tseq, d = x.shape
    t = nseq * tseq
    xf = x.reshape(t, d)
    qa, ka, va, ga, gk, qb16, kb, vb, kb16, vb16 = _in_proj(
        xf, w["norm1_g"], w["w_main"], w["w_la"], w["w_gk2p"], w["b_gk"], _block_rows(t, 256))
    a16, state = _gla(qa, ka, va, gk, ga, w["gla_norm_g"], state0, gla_chunks)
    if k_past16 is None:
        k_all, v_all = kb16, vb16
    else:
        k_all = jnp.concatenate([k_past16, kb16.reshape(nseq, tseq, SB_W)], axis=1).reshape(-1, SB_W)
        v_all = jnp.concatenate([v_past16, vb16.reshape(nseq, tseq, SB_W)], axis=1).reshape(-1, SB_W)
    s16 = _sb(qb16, k_all, v_all, w["sb_norm_g"], nseq, sb_blk)
    h, hn16, scores_t = _mid(xf, a16, s16, w["w_out16"], w["norm2_g"], w["w_query16"], w["sub_keys16"],
                             _block_rows(t, 256))
    e0, nsel, e1, rank1 = _route(scores_t, LANES)
    y = _experts(hn16, w["u16"], w["v16"], e0, nsel, e1, rank1, h, w["final_g"], _block_rows(t, 512), 512)
    return (y.reshape(nseq, tseq, d), state,
            kb.reshape(nseq, tseq, SB_HEADS, SB_HD), vb.reshape(nseq, tseq, SB_HEADS, SB_HD))


def kernel(x_prompt, x_sample, cache_k, cache_v, state_gla, norm1_g, w_in, w_gk2, b_gk, gla_norm_g, sb_norm_g,
           w_out, norm2_g, w_query, sub_keys, expert_u, expert_v, final_g):
    d = x_prompt.shape[-1]
    la0 = 2 * GLA_QK + 2 * GLA_V
    w_main = jnp.concatenate([w_in[:, :la0], w_in[:, la0 + GLA_LOWRANK:]], axis=1).astype(BF16)
    w_la = jnp.pad(w_in[:, la0:la0 + GLA_LOWRANK], ((0, 0), (0, LANES - GLA_LOWRANK)))
    w_gk2p = jnp.pad(w_gk2, ((0, LANES - GLA_LOWRANK), (0, 0)))
    w = dict(
        norm1_g=norm1_g.reshape(1, d), w_main=w_main, w_la=w_la, w_gk2p=w_gk2p, b_gk=b_gk.reshape(1, GLA_QK),
        gla_norm_g=gla_norm_g.reshape(1, GLA_DV), sb_norm_g=sb_norm_g.reshape(1, SB_HD),
        w_out16=w_out.astype(BF16), norm2_g=norm2_g.reshape(1, d), w_query16=w_query.astype(BF16),
        sub_keys16=sub_keys.astype(BF16), u16=expert_u.astype(BF16), v16=expert_v.astype(BF16),
        final_g=final_g.reshape(1, d))

    nb = x_prompt.shape[0]
    zero_state = jnp.zeros((nb, GLA_HEADS, GLA_DK, GLA_DV), F32)
    y_p, state_p, k_p, v_p = _group(x_prompt, zero_state, None, None, w, gla_chunks=4, sb_blk=256)

    ns, tpast = cache_k.shape[0], cache_k.shape[1]
    y_s, state_s, k_s, v_s = _group(
        x_sample, state_gla.astype(F32), cache_k.reshape(ns, tpast, SB_W).astype(BF16),
        cache_v.reshape(ns, tpast, SB_W).astype(BF16), w, gla_chunks=1, sb_blk=x_sample.shape[1])
    return (y_p, y_s, state_p, k_p, v_p, state_s, k_s, v_s)
```

```python
import functools
import math

import jax
import jax.numpy as jnp
from jax import lax
from jax.experimental import pallas as pl
from jax.experimental.pallas import tpu as pltpu

F32 = jnp.float32
BF16 = jnp.bfloat16

EPS = 1e-6
CHUNK = 64
SUB = 16
GLA_HEADS = 4
GLA_DK = 128
GLA_DV = 256
GLA_LOWRANK = 16
GLA_GATE_NORM = 16.0
SB_HEADS = 8
SB_HD = 128
PEER_HEADS = 8
PEER_NKEYS = 128
PEER_DQ = 256
PEER_TOPK = 16

GLA_QK = GLA_HEADS * GLA_DK
GLA_V = GLA_HEADS * GLA_DV
SB_W = SB_HEADS * SB_HD

LANES = 128
EXP_ZERO_BELOW = -105.0
VMEM_LIMIT = 56 * 1024 * 1024

_HIGHEST = lax.Precision.HIGHEST


def _dot(a, b, precision=None):
    return lax.dot_general(a, b, (((1,), (0,)), ((), ())), precision=precision,
                           preferred_element_type=F32)


def _dot_nt(a, b, precision=None):
    return lax.dot_general(a, b, (((1,), (1,)), ((), ())), precision=precision,
                           preferred_element_type=F32)


def _softplus_neg_abs(z):
    return jnp.log1p(jnp.exp(-jnp.abs(z)))


def _resident(shape):
    nd = len(shape)
    return pl.BlockSpec(shape, lambda *_: (0,) * nd, pipeline_mode=pl.Buffered(1))


def _params(sem):
    return pltpu.CompilerParams(dimension_semantics=sem, vmem_limit_bytes=VMEM_LIMIT)


def _in_proj_kernel(x_ref, g_ref, w_ref, wla_ref, wgk_ref, bgk_ref,
                    qa_ref, ka_ref, va_ref, ga_ref, gk_ref, qb_ref, kb_ref, vb_ref, kb16_ref, vb16_ref):
    x = x_ref[...]
    xn = x * lax.rsqrt(jnp.mean(x * x, axis=-1, keepdims=True) + EPS) * g_ref[...]
    xb = xn.astype(BF16)

    def proj(lo, width):
        return _dot(xb, w_ref[:, lo:lo + width])

    c = 0
    qa_ref[...] = proj(c, GLA_QK) * (GLA_DK ** -0.5); c += GLA_QK
    ka_ref[...] = proj(c, GLA_QK); c += GLA_QK
    va_ref[...] = proj(c, GLA_V); c += GLA_V
    ga_ref[...] = proj(c, GLA_V); c += GLA_V
    qb_ref[...] = (proj(c, SB_W) * (1.0 / math.sqrt(SB_HD))).astype(BF16); c += SB_W
    kb = proj(c, SB_W); c += SB_W
    vb = proj(c, SB_W)
    for head in range(SB_HEADS):
        cols = slice(head * SB_HD, (head + 1) * SB_HD)
        kb_ref[:, head, :] = kb[:, cols]
        vb_ref[:, head, :] = vb[:, cols]
    kb16_ref[...] = kb.astype(BF16)
    vb16_ref[...] = vb.astype(BF16)
    la = _dot(xn, wla_ref[...], _HIGHEST)
    pre = _dot(la, wgk_ref[...], _HIGHEST) + bgk_ref[...]
    gk_ref[...] = (jnp.minimum(pre, 0.0) - _softplus_neg_abs(pre)) * (1.0 / GLA_GATE_NORM)


def _in_proj(x, norm1_g, w_main, w_la, w_gk2p, b_gk, tb):
    t, d = x.shape
    nb = t // tb
    row = lambda w: pl.BlockSpec((tb, w), lambda i: (i, 0))
    heads = pl.BlockSpec((tb, SB_HEADS, SB_HD), lambda i: (i, 0, 0))
    flat = lambda w, dt: (row(w), jax.ShapeDtypeStruct((t, w), dt))
    per_head = (heads, jax.ShapeDtypeStruct((t, SB_HEADS, SB_HD), F32))
    outs = [flat(GLA_QK, F32), flat(GLA_QK, F32), flat(GLA_V, F32), flat(GLA_V, F32), flat(GLA_QK, F32),
            flat(SB_W, BF16), per_head, per_head, flat(SB_W, BF16), flat(SB_W, BF16)]
    return pl.pallas_call(
        _in_proj_kernel,
        grid=(nb,),
        in_specs=[row(d), _resident((1, d)), _resident(w_main.shape), _resident(w_la.shape),
                  _resident(w_gk2p.shape), _resident((1, GLA_QK))],
        out_specs=[spec for spec, _ in outs],
        out_shape=[shape for _, shape in outs],
        compiler_params=_params(("parallel",)),
        name="in_proj",
    )(x, norm1_g, w_main, w_la, w_gk2p, b_gk)


def _gla_chunk(q, k, v, gk, st):
    c = CHUNK
    nsub = c // SUB
    row = lax.broadcasted_iota(jnp.int32, (c, c), 0)
    col = lax.broadcasted_iota(jnp.int32, (c, c), 1)
    tril = (row >= col).astype(F32)
    b = _dot(tril, gk, _HIGHEST)
    b_last = b[c - 1:c, :]
    o = _dot_nt((q * jnp.exp(b)).astype(BF16), st.astype(BF16))

    b3 = b.reshape(nsub, SUB, GLA_DK)
    q3 = q.reshape(nsub, SUB, GLA_DK)
    k3 = k.reshape(nsub, SUB, GLA_DK)
    blk = lax.broadcasted_iota(jnp.int32, (nsub, SUB, c), 0)
    trow = lax.broadcasted_iota(jnp.int32, (nsub, SUB, c), 1)
    lane = lax.broadcasted_iota(jnp.int32, (nsub, SUB, c), 2)
    key_in_blk = lane - SUB * blk
    diag = jnp.zeros((nsub, SUB, c), F32)
    for s in range(SUB):
        dec = jnp.exp(jnp.minimum(b3 - b3[:, s:s + 1, :], 0.0))
        colsum = jnp.sum(q3 * dec * k3[:, s:s + 1, :], axis=-1, keepdims=True)
        diag = jnp.where(key_in_blk == s, jnp.where(trow >= s, colsum, 0.0), diag)
    diag = diag.reshape(c, c)

    lane2 = lax.broadcasted_iota(jnp.int32, (SUB, c), 1)
    rows = [diag[0:SUB, :]]
    for i in range(1, nsub):
        ref_b = b[i * SUB:i * SUB + 1, :]
        qi = q[i * SUB:(i + 1) * SUB, :] * jnp.exp(b[i * SUB:(i + 1) * SUB, :] - ref_b)
        ki = k * jnp.exp(jnp.minimum(ref_b - b, 0.0))
        p = _dot_nt(qi.astype(BF16), ki.astype(BF16))
        rows.append(jnp.where(lane2 < i * SUB, p, diag[i * SUB:(i + 1) * SUB, :]))
    scores = jnp.concatenate(rows, axis=0)
    vb = v.astype(BF16)
    o = o + _dot(scores.astype(BF16), vb)

    kd = (k * jnp.exp(b_last - b)).astype(BF16)
    st_new = st * jnp.exp(b_last) + _dot(v.T.astype(BF16), kd)
    return o, st_new


def _gla_kernel(q_ref, k_ref, v_ref, gk_ref, ga_ref, gn_ref, s0_ref, a_ref, sfin_ref, st_ref, *, nchunk):
    g = pl.program_id(2)

    @pl.when(g == 0)
    def _():
        st_ref[...] = s0_ref[0, 0].T

    st = st_ref[...]
    for c in range(nchunk):
        r = slice(c * CHUNK, (c + 1) * CHUNK)
        o, st = _gla_chunk(q_ref[r, :], k_ref[r, :], v_ref[r, :], gk_ref[r, :], st)
        on = o * lax.rsqrt(jnp.mean(o * o, axis=-1, keepdims=True) + EPS) * gn_ref[...]
        ga = ga_ref[r, :]
        a_ref[r, :] = (on * (ga / (1.0 + jnp.exp(-ga)))).astype(BF16)
    st_ref[...] = st

    @pl.when(g == pl.num_programs(2) - 1)
    def _():
        sfin_ref[0, 0] = st.T


def _gla(qa, ka, va, gk, ga, gla_norm_g, s0, nchunk):
    t = qa.shape[0]
    nseq = s0.shape[0]
    tg = nchunk * CHUNK
    ng = t // nseq // tg
    tok = lambda w: pl.BlockSpec((tg, w), lambda h, s, g: (s * ng + g, h))
    state = pl.BlockSpec((1, 1, GLA_DK, GLA_DV), lambda h, s, g: (s, h, 0, 0))
    return pl.pallas_call(
        functools.partial(_gla_kernel, nchunk=nchunk),
        grid=(GLA_HEADS, nseq, ng),
        in_specs=[tok(GLA_DK), tok(GLA_DK), tok(GLA_DV), tok(GLA_DK), tok(GLA_DV),
                  pl.BlockSpec((1, GLA_DV), lambda h, s, g: (0, 0)), state],
        out_specs=[tok(GLA_DV), state],
        out_shape=[jax.ShapeDtypeStruct((t, GLA_V), BF16),
                   jax.ShapeDtypeStruct((nseq, GLA_HEADS, GLA_DK, GLA_DV), F32)],
        scratch_shapes=[pltpu.VMEM((GLA_DV, GLA_DK), F32)],
        compiler_params=_params(("parallel", "arbitrary", "arbitrary")),
        name="gla",
    )(qa, ka, va, gk, ga, gla_norm_g, s0)


def _sb_kernel(q_ref, k_ref, v_ref, gn_ref, o_ref, acc_ref, carry_ref, *, blk, first_blk):
    i = pl.program_id(2)
    q = q_ref[...]
    row = lax.broadcasted_iota(jnp.int32, (blk, blk), 0)
    col = lax.broadcasted_iota(jnp.int32, (blk, blk), 1)
    later = (row > col).astype(BF16)
    visible = col < row

    def step(kb, masked):
        start = pl.multiple_of(kb * blk, blk)
        z = _dot_nt(q, k_ref[pl.ds(start, blk), :])
        sp = _softplus_neg_abs(z)
        log_beta = jnp.minimum(z, 0.0) - sp
        log_1mb = -(jnp.maximum(z, 0.0) + sp)
        if masked:
            log_1mb = jnp.where(visible, log_1mb, 0.0)
        hi = log_1mb.astype(BF16)
        lo = (log_1mb - hi.astype(F32)).astype(BF16)
        tail = _dot(hi, later) + _dot(lo, later)
        carry = carry_ref[...]
        a = jnp.exp(log_beta + tail + carry)
        if masked:
            a = jnp.where(visible, a, 0.0)
        acc_ref[...] += _dot(a.astype(BF16), v_ref[pl.ds(start, blk), :])
        carry = carry + tail[:, 0:1] + log_1mb[:, 0:1]
        carry_ref[...] = carry
        return jnp.max(carry)

    acc_ref[...] = jnp.zeros_like(acc_ref)
    carry_ref[...] = jnp.zeros_like(carry_ref)
    top = step(first_blk + i, True)

    def cond(state):
        kb, mx = state
        return jnp.logical_and(kb >= 0, mx > EXP_ZERO_BELOW)

    def body(state):
        kb, _ = state
        return kb - 1, step(kb, False)

    lax.while_loop(cond, body, (first_blk + i - 1, top))

    o = acc_ref[...]
    on = o * lax.rsqrt(jnp.mean(o * o, axis=-1, keepdims=True) + EPS) * gn_ref[...]
    o_ref[...] = on.astype(BF16)


def _sb(q16, k16, v16, sb_norm_g, nseq, blk):
    tq = q16.shape[0] // nseq
    tk = k16.shape[0] // nseq
    nq = tq // blk
    first_blk = (tk - tq) // blk
    qspec = pl.BlockSpec((blk, SB_HD), lambda s, h, i: (s * nq + i, h))
    kspec = pl.BlockSpec((tk, SB_HD), lambda s, h, i: (s, h))
    return pl.pallas_call(
        functools.partial(_sb_kernel, blk=blk, first_blk=first_blk),
        grid=(nseq, SB_HEADS, nq),
        in_specs=[qspec, kspec, kspec, pl.BlockSpec((1, SB_HD), lambda s, h, i: (0, 0))],
        out_specs=qspec,
        out_shape=jax.ShapeDtypeStruct(q16.shape, BF16),
        scratch_shapes=[pltpu.VMEM((blk, SB_HD), F32), pltpu.VMEM((blk, 1), F32)],
        compiler_params=_params(("parallel", "parallel", "arbitrary")),
        name="sb",
    )(q16, k16, v16, sb_norm_g)


def _mid_kernel(x_ref, a_ref, s_ref, wo_ref, g2_ref, wq_ref, sk_ref, h_ref, hn_ref, st_ref):
    h = x_ref[...] + _dot(a_ref[...], wo_ref[0:GLA_V, :]) + _dot(s_ref[...], wo_ref[GLA_V:GLA_V + SB_W, :])
    h_ref[...] = h
    hn = (h * lax.rsqrt(jnp.mean(h * h, axis=-1, keepdims=True) + EPS) * g2_ref[...]).astype(BF16)
    hn_ref[...] = hn
    q = _dot(hn, wq_ref[...])
    half = PEER_DQ // 2
    for hp in range(PEER_HEADS * 2):
        head, part = divmod(hp, 2)
        qs = q[:, hp * half:(hp + 1) * half].astype(BF16)
        st_ref[hp * PEER_NKEYS:(hp + 1) * PEER_NKEYS, :] = _dot_nt(sk_ref[part, head], qs)


def _mid(x, a16, s16, w_out16, norm2_g, w_query16, sub_keys16, tb):
    t, d = x.shape
    nb = t // tb
    row = lambda w: pl.BlockSpec((tb, w), lambda i: (i, 0))
    nrow = PEER_HEADS * 2 * PEER_NKEYS
    return pl.pallas_call(
        _mid_kernel,
        grid=(nb,),
        in_specs=[row(d), row(GLA_V), row(SB_W), _resident(w_out16.shape), _resident((1, d)),
                  _resident(w_query16.shape), _resident(sub_keys16.shape)],
        out_specs=[row(d), row(d), pl.BlockSpec((nrow, tb), lambda i: (0, i))],
        out_shape=[jax.ShapeDtypeStruct((t, d), F32), jax.ShapeDtypeStruct((t, d), BF16),
                   jax.ShapeDtypeStruct((nrow, t), F32)],
        compiler_params=_params(("parallel",)),
        name="mid",
    )(x, a16, s16, w_out16, norm2_g, w_query16, sub_keys16)


NOT_SELECTED = 99.0


def _top_rows(s, k, order):
    rank = jnp.full(s.shape, NOT_SELECTED, F32)
    work = s
    vals = []
    for r in range(k):
        m = jnp.max(work, axis=0, keepdims=True)
        first = jnp.min(jnp.where(work == m, order, jnp.inf), axis=0, keepdims=True)
        hit = order == first
        rank = jnp.where(hit, float(r), rank)
        work = jnp.where(hit, -jnp.inf, work)
        vals.append(m)
    return jnp.concatenate(vals, axis=0), rank


def _route_kernel(st_ref, e0_ref, n_ref, e1_ref, r_ref):
    k = PEER_TOPK
    nk = PEER_NKEYS
    tl = st_ref.shape[1]
    key_order = lax.broadcasted_iota(jnp.int32, (nk, tl), 0).astype(F32)
    h = k // 2
    nc = k + (h - 1) * h + h
    row = lax.broadcasted_iota(jnp.int32, (nc, tl), 0)
    assert h & (h - 1) == 0
    mid_a = 1 + jnp.right_shift(row - k, h.bit_length() - 1)
    mid_b = jnp.bitwise_and(row - k, h - 1)
    cand_order = jnp.where(row < k, row,
                           jnp.where(row < nc - h, mid_a * k + mid_b, (row - (nc - k)) * k)).astype(F32)
    for head in range(PEER_HEADS):
        s0 = st_ref[(2 * head) * nk:(2 * head + 1) * nk, :]
        s1 = st_ref[(2 * head + 1) * nk:(2 * head + 2) * nk, :]
        v0, rank0 = _top_rows(s0, k, key_order)
        v1, rank1 = _top_rows(s1, k, key_order)
        cand = jnp.concatenate([v0[0:1, :] + v1] + [v0[a:a + 1, :] + v1[0:h, :] for a in range(1, h)]
                               + [v0[h:k, :] + v1[0:1, :]], axis=0)
        best, crank = _top_rows(cand, k, cand_order)
        chosen = (crank < float(k)).astype(F32)
        counts = [jnp.sum(chosen[0:k, :], axis=0, keepdims=True)]
        counts += [jnp.sum(chosen[k + (a - 1) * h:k + a * h, :], axis=0, keepdims=True) for a in range(1, h)]
        counts += [chosen[nc - h + a:nc - h + a + 1, :] for a in range(h)]
        z = jnp.sum(jnp.exp(best - best[0:1, :]), axis=0, keepdims=True)
        nsel = jnp.zeros_like(s0)
        for a in range(k):
            nsel = jnp.where(rank0 == float(a), counts[a], nsel)
        e0_ref[:, head, :] = jnp.exp(s0 - v0[0:1, :]) / z
        n_ref[:, head, :] = nsel
        rows = slice(head * nk, (head + 1) * nk)
        e1_ref[rows, :] = jnp.exp(s1 - v1[0:1, :]).astype(BF16)
        r_ref[rows, :] = rank1.astype(BF16)


def _route(scores_t, tl):
    nrow, t = scores_t.shape
    by_key = pl.BlockSpec((PEER_NKEYS, PEER_HEADS, tl), lambda i: (0, 0, i))
    by_head = pl.BlockSpec((nrow // 2, tl), lambda i: (0, i))
    return pl.pallas_call(
        _route_kernel,
        grid=(t // tl,),
        in_specs=[pl.BlockSpec((nrow, tl), lambda i: (0, i))],
        out_specs=[by_key, by_key, by_head, by_head],
        out_shape=[jax.ShapeDtypeStruct((PEER_NKEYS, PEER_HEADS, t), F32)] * 2
        + [jax.ShapeDtypeStruct((nrow // 2, t), BF16)] * 2,
        compiler_params=_params(("parallel",)),
        name="route",
    )(scores_t)


BF16_ROWS = 16


def _experts_kernel(hn_ref, u_ref, vt_ref, e0_ref, n_ref, e1_ref, r_ref, h_ref, gf_ref, y_ref, acc_ref, pt_ref, *, ec):
    e = pl.program_id(1)
    nk = PEER_NKEYS
    tb = hn_ref.shape[0]
    slot = lax.rem(e, 2)

    @pl.when(e == 0)
    def _():
        acc_ref[...] = jnp.zeros_like(acc_ref)
        pt_ref[1] = jnp.zeros(pt_ref.shape[1:], BF16)

    pre = _dot_nt(u_ref[...], hn_ref[...])
    acc_ref[...] += _dot(vt_ref[...], pt_ref[1 - slot])
    packed = (nk // BF16_ROWS, BF16_ROWS, tb)
    for ii in range(ec // nk):
        w = jnp.zeros(packed, BF16)
        for head in range(PEER_HEADS):
            e0 = jnp.broadcast_to(e0_ref[ii, head:head + 1, :], (BF16_ROWS, tb)).astype(BF16)[None]
            n0 = jnp.broadcast_to(n_ref[ii, head:head + 1, :], (BF16_ROWS, tb)).astype(BF16)[None]
            rows = slice(head * nk, (head + 1) * nk)
            e1 = e1_ref[rows, :].reshape(packed)
            r1 = r_ref[rows, :].reshape(packed)
            w = w + jnp.where(r1 < n0, e0 * e1, jnp.zeros((), BF16))
        x = pre[ii * nk:(ii + 1) * nk, :]
        act = 0.5 * x * (1.0 + lax.erf(x * (1.0 / math.sqrt(2.0))))
        pt_ref[slot, ii * nk:(ii + 1) * nk, :] = w.reshape(nk, tb) * act.astype(BF16)

    @pl.when(e == pl.num_programs(1) - 1)
    def _():
        h2 = h_ref[...] + acc_ref[...].T
        y_ref[...] = h2 * lax.rsqrt(jnp.mean(h2 * h2, axis=-1, keepdims=True) + EPS) * gf_ref[...]


def _experts(hn16, u16, vt16, e0, nsel, e1, rank1, h, final_g, tb, ec):
    t, d = h.shape
    nchunk = u16.shape[0] // ec
    this = lambda e: jnp.minimum(e, nchunk - 1)
    prev = lambda e: jnp.maximum(e - 1, 0)
    tok = pl.BlockSpec((tb, d), lambda i, e: (i, 0))
    by_key = pl.BlockSpec((ec // PEER_NKEYS, PEER_HEADS, tb), lambda i, e: (this(e), 0, i))
    by_head = pl.BlockSpec((e1.shape[0], tb), lambda i, e: (0, i))
    return pl.pallas_call(
        functools.partial(_experts_kernel, ec=ec),
        grid=(t // tb, nchunk + 1),
        in_specs=[tok, pl.BlockSpec((ec, d), lambda i, e: (this(e), 0)),
                  pl.BlockSpec((d, ec), lambda i, e: (0, prev(e))),
                  by_key, by_key, by_head, by_head,
                  pl.BlockSpec((tb, d), lambda i, e: (i, 0), pipeline_mode=pl.Buffered(1)),
                  pl.BlockSpec((1, d), lambda i, e: (0, 0))],
        out_specs=tok,
        out_shape=jax.ShapeDtypeStruct((t, d), F32),
        scratch_shapes=[pltpu.VMEM((d, tb), F32), pltpu.VMEM((2, ec, tb), BF16)],
        compiler_params=_params(("parallel", "arbitrary")),
        name="experts",
    )(hn16, u16, vt16, e0, nsel, e1, rank1, h, final_g)


def _block_rows(t, want):
    b = min(want, t)
    while t % b:
        b //= 2
    return b


def _group(x, state0, k_past16, v_past16, w, gla_chunks, sb_blk):
    nseq, tseq, d = x.shape
    t = nseq * tseq
    xf = x.reshape(t, d)
    qa, ka, va, ga, gk, qb16, kb, vb, kb16, vb16 = _in_proj(
        xf, w["norm1_g"], w["w_main"], w["w_la"], w["w_gk2p"], w["b_gk"], _block_rows(t, 256))
    a16, state = _gla(qa, ka, va, gk, ga, w["gla_norm_g"], state0, gla_chunks)
    if k_past16 is None:
        k_all, v_all = kb16, vb16
    else:
        k_all = jnp.concatenate([k_past16, kb16.reshape(nseq, tseq, SB_W)], axis=1).reshape(-1, SB_W)
        v_all = jnp.concatenate([v_past16, vb16.reshape(nseq, tseq, SB_W)], axis=1).reshape(-1, SB_W)
    s16 = _sb(qb16, k_all, v_all, w["sb_norm_g"], nseq, sb_blk)
    h, hn16, scores_t = _mid(xf, a16, s16, w["w_out16"], w["norm2_g"], w["w_query16"], w["sub_keys16"],
                             _block_rows(t, 256))
    e0, nsel, e1, rank1 = _route(scores_t, LANES)
    y = _experts(hn16, w["u16"], w["vt16"], e0, nsel, e1, rank1, h, w["final_g"], _block_rows(t, 512), 1024)
    return (y.reshape(nseq, tseq, d), state,
            kb.reshape(nseq, tseq, SB_HEADS, SB_HD), vb.reshape(nseq, tseq, SB_HEADS, SB_HD))


def kernel(x_prompt, x_sample, cache_k, cache_v, state_gla, norm1_g, w_in, w_gk2, b_gk, gla_norm_g, sb_norm_g,
           w_out, norm2_g, w_query, sub_keys, expert_u, expert_v, final_g):
    d = x_prompt.shape[-1]
    la0 = 2 * GLA_QK + 2 * GLA_V
    w_main = jnp.concatenate([w_in[:, :la0], w_in[:, la0 + GLA_LOWRANK:]], axis=1).astype(BF16)
    w_la = jnp.pad(w_in[:, la0:la0 + GLA_LOWRANK], ((0, 0), (0, LANES - GLA_LOWRANK)))
    w_gk2p = jnp.pad(w_gk2, ((0, LANES - GLA_LOWRANK), (0, 0)))
    w = dict(
        norm1_g=norm1_g.reshape(1, d), w_main=w_main, w_la=w_la, w_gk2p=w_gk2p, b_gk=b_gk.reshape(1, GLA_QK),
        gla_norm_g=gla_norm_g.reshape(1, GLA_DV), sb_norm_g=sb_norm_g.reshape(1, SB_HD),
        w_out16=w_out.astype(BF16), norm2_g=norm2_g.reshape(1, d), w_query16=w_query.astype(BF16),
        sub_keys16=sub_keys.astype(BF16), u16=expert_u.astype(BF16), vt16=expert_v.T.astype(BF16),
        final_g=final_g.reshape(1, d))

    nb = x_prompt.shape[0]
    zero_state = jnp.zeros((nb, GLA_HEADS, GLA_DK, GLA_DV), F32)
    y_p, state_p, k_p, v_p = _group(x_prompt, zero_state, None, None, w, gla_chunks=4, sb_blk=256)

    ns, tpast = cache_k.shape[0], cache_k.shape[1]
    y_s, state_s, k_s, v_s = _group(
        x_sample, state_gla.astype(F32), cache_k.reshape(ns, tpast, SB_W).astype(BF16),
        cache_v.reshape(ns, tpast, SB_W).astype(BF16), w, gla_chunks=1, sb_blk=x_sample.shape[1])
    return (y_p, y_s, state_p, k_p, v_p, state_s, k_s, v_s)
```

```python
import functools
import math

import jax
import jax.numpy as jnp
from jax import lax
from jax.experimental import pallas as pl
from jax.experimental.pallas import tpu as pltpu

F32 = jnp.float32
BF16 = jnp.bfloat16

EPS = 1e-6
CHUNK = 64
SUB = 16
GLA_HEADS = 4
GLA_DK = 128
GLA_DV = 256
GLA_LOWRANK = 16
GLA_GATE_NORM = 16.0
SB_HEADS = 8
SB_HD = 128
PEER_HEADS = 8
PEER_NKEYS = 128
PEER_DQ = 256
PEER_TOPK = 16

GLA_QK = GLA_HEADS * GLA_DK
GLA_V = GLA_HEADS * GLA_DV
SB_W = SB_HEADS * SB_HD

LANES = 128
EXP_ZERO_BELOW = -105.0
VMEM_LIMIT = 56 * 1024 * 1024

_HIGHEST = lax.Precision.HIGHEST


def _dot(a, b, precision=None):
    return lax.dot_general(a, b, (((1,), (0,)), ((), ())), precision=precision,
                           preferred_element_type=F32)


def _dot_nt(a, b, precision=None):
    return lax.dot_general(a, b, (((1,), (1,)), ((), ())), precision=precision,
                           preferred_element_type=F32)


def _softplus_neg_abs(z):
    return jnp.log1p(jnp.exp(-jnp.abs(z)))


def _resident(shape):
    nd = len(shape)
    return pl.BlockSpec(shape, lambda *_: (0,) * nd, pipeline_mode=pl.Buffered(1))


def _params(sem):
    return pltpu.CompilerParams(dimension_semantics=sem, vmem_limit_bytes=VMEM_LIMIT)


def _in_proj_kernel(x_ref, g_ref, w_ref, wla_ref, wgk_ref, bgk_ref,
                    qa_ref, ka_ref, va_ref, ga_ref, gk_ref, qb_ref, kb_ref, vb_ref, kb16_ref, vb16_ref):
    x = x_ref[...]
    xn = x * lax.rsqrt(jnp.mean(x * x, axis=-1, keepdims=True) + EPS) * g_ref[...]
    xb = xn.astype(BF16)

    def proj(lo, width):
        return _dot(xb, w_ref[:, lo:lo + width])

    c = 0
    qa_ref[...] = proj(c, GLA_QK) * (GLA_DK ** -0.5); c += GLA_QK
    ka_ref[...] = proj(c, GLA_QK); c += GLA_QK
    va_ref[...] = proj(c, GLA_V); c += GLA_V
    ga_ref[...] = proj(c, GLA_V); c += GLA_V
    sb = proj(c, GLA_LOWRANK + 3 * SB_W)
    c = GLA_LOWRANK
    qb_ref[...] = (sb[:, c:c + SB_W] * (1.0 / math.sqrt(SB_HD))).astype(BF16); c += SB_W
    kb = sb[:, c:c + SB_W]; c += SB_W
    vb = sb[:, c:c + SB_W]
    kb_ref[...] = kb
    vb_ref[...] = vb
    kb16_ref[...] = kb.astype(BF16)
    vb16_ref[...] = vb.astype(BF16)
    la = _dot(xn, wla_ref[...], _HIGHEST)
    pre = _dot(la, wgk_ref[...], _HIGHEST) + bgk_ref[...]
    gk_ref[...] = (jnp.minimum(pre, 0.0) - _softplus_neg_abs(pre)) * (1.0 / GLA_GATE_NORM)


def _in_proj(x, norm1_g, w_in16, w_la, w_gk2p, b_gk, tb):
    t, d = x.shape
    nb = t // tb
    row = lambda w: pl.BlockSpec((tb, w), lambda i: (i, 0))
    outs = [(GLA_QK, F32), (GLA_QK, F32), (GLA_V, F32), (GLA_V, F32), (GLA_QK, F32),
            (SB_W, BF16), (SB_W, F32), (SB_W, F32), (SB_W, BF16), (SB_W, BF16)]
    return pl.pallas_call(
        _in_proj_kernel,
        grid=(nb,),
        in_specs=[row(d), _resident((1, d)), _resident(w_in16.shape), _resident(w_la.shape),
                  _resident(w_gk2p.shape), _resident((1, GLA_QK))],
        out_specs=[row(w) for w, _ in outs],
        out_shape=[jax.ShapeDtypeStruct((t, w), dt) for w, dt in outs],
        compiler_params=_params(("parallel",)),
        name="in_proj",
    )(x, norm1_g, w_in16, w_la, w_gk2p, b_gk)


def _gla_chunk(q, k, v, gk, st):
    c = CHUNK
    nsub = c // SUB
    row = lax.broadcasted_iota(jnp.int32, (c, c), 0)
    col = lax.broadcasted_iota(jnp.int32, (c, c), 1)
    tril = (row >= col).astype(F32)
    b = _dot(tril, gk, _HIGHEST)
    b_last = b[c - 1:c, :]
    o = _dot_nt((q * jnp.exp(b)).astype(BF16), st.astype(BF16))

    b3 = b.reshape(nsub, SUB, GLA_DK)
    q3 = q.reshape(nsub, SUB, GLA_DK)
    k3 = k.reshape(nsub, SUB, GLA_DK)
    blk = lax.broadcasted_iota(jnp.int32, (nsub, SUB, c), 0)
    trow = lax.broadcasted_iota(jnp.int32, (nsub, SUB, c), 1)
    lane = lax.broadcasted_iota(jnp.int32, (nsub, SUB, c), 2)
    key_in_blk = lane - SUB * blk
    diag = jnp.zeros((nsub, SUB, c), F32)
    for s in range(SUB):
        dec = jnp.exp(jnp.minimum(b3 - b3[:, s:s + 1, :], 0.0))
        colsum = jnp.sum(q3 * dec * k3[:, s:s + 1, :], axis=-1, keepdims=True)
        diag = jnp.where(key_in_blk == s, jnp.where(trow >= s, colsum, 0.0), diag)
    diag = diag.reshape(c, c)

    lane2 = lax.broadcasted_iota(jnp.int32, (SUB, c), 1)
    rows = [diag[0:SUB, :]]
    for i in range(1, nsub):
        ref_b = b[i * SUB:i * SUB + 1, :]
        qi = q[i * SUB:(i + 1) * SUB, :] * jnp.exp(b[i * SUB:(i + 1) * SUB, :] - ref_b)
        ki = k * jnp.exp(jnp.minimum(ref_b - b, 0.0))
        p = _dot_nt(qi.astype(BF16), ki.astype(BF16))
        rows.append(jnp.where(lane2 < i * SUB, p, diag[i * SUB:(i + 1) * SUB, :]))
    scores = jnp.concatenate(rows, axis=0)
    vb = v.astype(BF16)
    o = o + _dot(scores.astype(BF16), vb)

    kd = (k * jnp.exp(b_last - b)).astype(BF16)
    st_new = st * jnp.exp(b_last) + _dot(v.T.astype(BF16), kd)
    return o, st_new


def _gla_kernel(q_ref, k_ref, v_ref, gk_ref, ga_ref, gn_ref, s0_ref, a_ref, sfin_ref, st_ref, *, nchunk):
    g = pl.program_id(2)

    @pl.when(g == 0)
    def _():
        st_ref[...] = s0_ref[0, 0].T

    st = st_ref[...]
    for c in range(nchunk):
        r = slice(c * CHUNK, (c + 1) * CHUNK)
        o, st = _gla_chunk(q_ref[r, :], k_ref[r, :], v_ref[r, :], gk_ref[r, :], st)
        on = o * lax.rsqrt(jnp.mean(o * o, axis=-1, keepdims=True) + EPS) * gn_ref[...]
        ga = ga_ref[r, :]
        a_ref[r, :] = (on * (ga / (1.0 + jnp.exp(-ga)))).astype(BF16)
    st_ref[...] = st

    @pl.when(g == pl.num_programs(2) - 1)
    def _():
        sfin_ref[0, 0] = st.T


def _gla(qa, ka, va, gk, ga, gla_norm_g, s0, nchunk):
    t = qa.shape[0]
    nseq = s0.shape[0]
    tg = nchunk * CHUNK
    ng = t // nseq // tg
    tok = lambda w: pl.BlockSpec((tg, w), lambda h, s, g: (s * ng + g, h))
    state = pl.BlockSpec((1, 1, GLA_DK, GLA_DV), lambda h, s, g: (s, h, 0, 0))
    return pl.pallas_call(
        functools.partial(_gla_kernel, nchunk=nchunk),
        grid=(GLA_HEADS, nseq, ng),
        in_specs=[tok(GLA_DK), tok(GLA_DK), tok(GLA_DV), tok(GLA_DK), tok(GLA_DV),
                  pl.BlockSpec((1, GLA_DV), lambda h, s, g: (0, 0)), state],
        out_specs=[tok(GLA_DV), state],
        out_shape=[jax.ShapeDtypeStruct((t, GLA_V), BF16),
                   jax.ShapeDtypeStruct((nseq, GLA_HEADS, GLA_DK, GLA_DV), F32)],
        scratch_shapes=[pltpu.VMEM((GLA_DV, GLA_DK), F32)],
        compiler_params=_params(("parallel", "arbitrary", "arbitrary")),
        name="gla",
    )(qa, ka, va, gk, ga, gla_norm_g, s0)


def _sb_add_block(q, k, v, acc_ref, carry_ref, own):
    nq, nk = q.shape[0], k.shape[0]
    later = (lax.broadcasted_iota(jnp.int32, (nk, nk), 0) > lax.broadcasted_iota(jnp.int32, (nk, nk), 1)).astype(BF16)
    z = _dot_nt(q, k)
    sp = _softplus_neg_abs(z)
    log_beta = jnp.minimum(z, 0.0) - sp
    log_1mb = -(jnp.maximum(z, 0.0) + sp)
    if own:
        visible = lax.broadcasted_iota(jnp.int32, (nq, nk), 1) < lax.broadcasted_iota(jnp.int32, (nq, nk), 0)
        log_1mb = jnp.where(visible, log_1mb, 0.0)
    hi = log_1mb.astype(BF16)
    lo = (log_1mb - hi.astype(F32)).astype(BF16)
    tail = _dot(hi, later) + _dot(lo, later)
    carry = carry_ref[...]
    a = jnp.exp(log_beta + tail + carry)
    if own:
        a = jnp.where(visible, a, 0.0)
    acc_ref[...] += _dot(a.astype(BF16), v)
    carry = carry + tail[:, 0:1] + log_1mb[:, 0:1]
    carry_ref[...] = carry
    return jnp.max(carry)


def _sb_walk_back(first, top, add_block):
    def cond(state):
        kb, mx = state
        return jnp.logical_and(kb >= 0, mx > EXP_ZERO_BELOW)

    def body(state):
        kb, _ = state
        return kb - 1, add_block(kb)

    lax.while_loop(cond, body, (first, top))


def _sb_finish(acc_ref, gn_ref, o_ref):
    o = acc_ref[...]
    on = o * lax.rsqrt(jnp.mean(o * o, axis=-1, keepdims=True) + EPS) * gn_ref[...]
    o_ref[...] = on.astype(BF16)


def _sb_kernel(q_ref, k_ref, v_ref, gn_ref, o_ref, acc_ref, carry_ref, *, blk):
    i = pl.program_id(2)
    q = q_ref[...]

    def add_block(kb, own=False):
        rows = pl.ds(pl.multiple_of(kb * blk, blk), blk)
        return _sb_add_block(q, k_ref[rows, :], v_ref[rows, :], acc_ref, carry_ref, own)

    acc_ref[...] = jnp.zeros_like(acc_ref)
    carry_ref[...] = jnp.zeros_like(carry_ref)
    top = add_block(i, own=True)
    _sb_walk_back(i - 1, top, add_block)
    _sb_finish(acc_ref, gn_ref, o_ref)


def _sb_cached_kernel(q_ref, kn_ref, vn_ref, kp_ref, vp_ref, gn_ref, o_ref, acc_ref, carry_ref, *, pblk):
    head = pl.program_id(1)
    q = q_ref[...]

    def add_past(kb):
        rows = pl.ds(kb * (pblk * SB_HEADS) + head, pblk, stride=SB_HEADS)
        return _sb_add_block(q, kp_ref[rows, :].astype(BF16), vp_ref[rows, :].astype(BF16), acc_ref, carry_ref, False)

    acc_ref[...] = jnp.zeros_like(acc_ref)
    carry_ref[...] = jnp.zeros_like(carry_ref)
    top = _sb_add_block(q, kn_ref[...], vn_ref[...], acc_ref, carry_ref, True)
    _sb_walk_back(kp_ref.shape[0] // (pblk * SB_HEADS) - 1, top, add_past)
    _sb_finish(acc_ref, gn_ref, o_ref)


def _sb(q16, k16, v16, sb_norm_g, nseq, blk):
    t = q16.shape[0] // nseq
    nq = t // blk
    qspec = pl.BlockSpec((blk, SB_HD), lambda s, h, i: (s * nq + i, h))
    kspec = pl.BlockSpec((t, SB_HD), lambda s, h, i: (s, h))
    return pl.pallas_call(
        functools.partial(_sb_kernel, blk=blk),
        grid=(nseq, SB_HEADS, nq),
        in_specs=[qspec, kspec, kspec, pl.BlockSpec((1, SB_HD), lambda s, h, i: (0, 0))],
        out_specs=qspec,
        out_shape=jax.ShapeDtypeStruct(q16.shape, BF16),
        scratch_shapes=[pltpu.VMEM((blk, SB_HD), F32), pltpu.VMEM((blk, 1), F32)],
        compiler_params=_params(("parallel", "parallel", "arbitrary")),
        name="sb",
    )(q16, k16, v16, sb_norm_g)


def _sb_cached(q16, k16, v16, cache_k, cache_v, sb_norm_g, pblk):
    nseq, tpast = cache_k.shape[0], cache_k.shape[1]
    assert tpast % pblk == 0
    tq = q16.shape[0] // nseq
    new = pl.BlockSpec((tq, SB_HD), lambda s, h: (s, h))
    past = pl.BlockSpec((None, tpast * SB_HEADS, SB_HD), lambda s, h: (s, 0, 0))
    cache_k = cache_k.reshape(nseq, tpast * SB_HEADS, SB_HD)
    cache_v = cache_v.reshape(nseq, tpast * SB_HEADS, SB_HD)
    return pl.pallas_call(
        functools.partial(_sb_cached_kernel, pblk=pblk),
        grid=(nseq, SB_HEADS),
        in_specs=[new, new, new, past, past, pl.BlockSpec((1, SB_HD), lambda s, h: (0, 0))],
        out_specs=new,
        out_shape=jax.ShapeDtypeStruct(q16.shape, BF16),
        scratch_shapes=[pltpu.VMEM((tq, SB_HD), F32), pltpu.VMEM((tq, 1), F32)],
        compiler_params=_params(("parallel", "parallel")),
        name="sb_cached",
    )(q16, k16, v16, cache_k, cache_v, sb_norm_g)


def _mid_kernel(x_ref, a_ref, s_ref, wo_ref, g2_ref, wq_ref, sk_ref, h_ref, hn_ref, st_ref):
    h = x_ref[...] + _dot(a_ref[...], wo_ref[0:GLA_V, :]) + _dot(s_ref[...], wo_ref[GLA_V:GLA_V + SB_W, :])
    h_ref[...] = h
    hn = (h * lax.rsqrt(jnp.mean(h * h, axis=-1, keepdims=True) + EPS) * g2_ref[...]).astype(BF16)
    hn_ref[...] = hn
    q = _dot(hn, wq_ref[...])
    half = PEER_DQ // 2
    for hp in range(PEER_HEADS * 2):
        head, part = divmod(hp, 2)
        qs = q[:, hp * half:(hp + 1) * half].astype(BF16)
        st_ref[hp * PEER_NKEYS:(hp + 1) * PEER_NKEYS, :] = _dot_nt(sk_ref[part, head], qs)


def _mid(x, a16, s16, w_out16, norm2_g, w_query16, sub_keys16, tb):
    t, d = x.shape
    nb = t // tb
    row = lambda w: pl.BlockSpec((tb, w), lambda i: (i, 0))
    nrow = PEER_HEADS * 2 * PEER_NKEYS
    return pl.pallas_call(
        _mid_kernel,
        grid=(nb,),
        in_specs=[row(d), row(GLA_V), row(SB_W), _resident(w_out16.shape), _resident((1, d)),
                  _resident(w_query16.shape), _resident(sub_keys16.shape)],
        out_specs=[row(d), row(d), pl.BlockSpec((nrow, tb), lambda i: (0, i))],
        out_shape=[jax.ShapeDtypeStruct((t, d), F32), jax.ShapeDtypeStruct((t, d), BF16),
                   jax.ShapeDtypeStruct((nrow, t), F32)],
        compiler_params=_params(("parallel",)),
        name="mid",
    )(x, a16, s16, w_out16, norm2_g, w_query16, sub_keys16)


NOT_SELECTED = 99.0


def _top_rows(s, k, order):
    rank = jnp.full(s.shape, NOT_SELECTED, F32)
    work = s
    vals = []
    for r in range(k):
        m = jnp.max(work, axis=0, keepdims=True)
        first = jnp.min(jnp.where(work == m, order, jnp.inf), axis=0, keepdims=True)
        hit = order == first
        rank = jnp.where(hit, float(r), rank)
        work = jnp.where(hit, -jnp.inf, work)
        vals.append(m)
    return jnp.concatenate(vals, axis=0), rank


def _route_kernel(st_ref, e0_ref, n_ref, e1_ref, r_ref):
    k = PEER_TOPK
    nk = PEER_NKEYS
    tl = st_ref.shape[1]
    key_order = lax.broadcasted_iota(jnp.int32, (nk, tl), 0).astype(F32)
    h = k // 2
    nc = k + (h - 1) * h + h
    row = lax.broadcasted_iota(jnp.int32, (nc, tl), 0)
    assert h & (h - 1) == 0
    mid_a = 1 + jnp.right_shift(row - k, h.bit_length() - 1)
    mid_b = jnp.bitwise_and(row - k, h - 1)
    cand_order = jnp.where(row < k, row,
                           jnp.where(row < nc - h, mid_a * k + mid_b, (row - (nc - k)) * k)).astype(F32)
    for head in range(PEER_HEADS):
        s0 = st_ref[(2 * head) * nk:(2 * head + 1) * nk, :]
        s1 = st_ref[(2 * head + 1) * nk:(2 * head + 2) * nk, :]
        v0, rank0 = _top_rows(s0, k, key_order)
        v1, rank1 = _top_rows(s1, k, key_order)
        cand = jnp.concatenate([v0[0:1, :] + v1] + [v0[a:a + 1, :] + v1[0:h, :] for a in range(1, h)]
                               + [v0[h:k, :] + v1[0:1, :]], axis=0)
        best, crank = _top_rows(cand, k, cand_order)
        chosen = (crank < float(k)).astype(F32)
        counts = [jnp.sum(chosen[0:k, :], axis=0, keepdims=True)]
        counts += [jnp.sum(chosen[k + (a - 1) * h:k + a * h, :], axis=0, keepdims=True) for a in range(1, h)]
        counts += [chosen[nc - h + a:nc - h + a + 1, :] for a in range(h)]
        z = jnp.sum(jnp.exp(best - best[0:1, :]), axis=0, keepdims=True)
        nsel = jnp.zeros_like(s0)
        for a in range(k):
            nsel = jnp.where(rank0 == float(a), counts[a], nsel)
        e0_ref[:, head, :] = jnp.exp(s0 - v0[0:1, :]) / z
        n_ref[:, head, :] = nsel
        rows = slice(head * nk, (head + 1) * nk)
        e1_ref[rows, :] = jnp.exp(s1 - v1[0:1, :]).astype(BF16)
        r_ref[rows, :] = rank1.astype(BF16)


def _route(scores_t, tl):
    nrow, t = scores_t.shape
    by_key = pl.BlockSpec((PEER_NKEYS, PEER_HEADS, tl), lambda i: (0, 0, i))
    by_head = pl.BlockSpec((nrow // 2, tl), lambda i: (0, i))
    return pl.pallas_call(
        _route_kernel,
        grid=(t // tl,),
        in_specs=[pl.BlockSpec((nrow, tl), lambda i: (0, i))],
        out_specs=[by_key, by_key, by_head, by_head],
        out_shape=[jax.ShapeDtypeStruct((PEER_NKEYS, PEER_HEADS, t), F32)] * 2
        + [jax.ShapeDtypeStruct((nrow // 2, t), BF16)] * 2,
        compiler_params=_params(("parallel",)),
        name="route",
    )(scores_t)


BF16_ROWS = 16


def _experts_kernel(hn_ref, u_ref, vt_ref, e0_ref, n_ref, e1_ref, r_ref, h_ref, gf_ref, y_ref, acc_ref, pt_ref, *, ec):
    e = pl.program_id(1)
    nk = PEER_NKEYS
    tb = hn_ref.shape[0]
    slot = lax.rem(e, 2)

    @pl.when(e == 0)
    def _():
        acc_ref[...] = jnp.zeros_like(acc_ref)
        pt_ref[1] = jnp.zeros(pt_ref.shape[1:], BF16)

    pre = _dot_nt(u_ref[...], hn_ref[...])
    acc_ref[...] += _dot(vt_ref[...], pt_ref[1 - slot])
    packed = (nk // BF16_ROWS, BF16_ROWS, tb)
    for ii in range(ec // nk):
        w = jnp.zeros(packed, BF16)
        for head in range(PEER_HEADS):
            e0 = jnp.broadcast_to(e0_ref[ii, head:head + 1, :], (BF16_ROWS, tb)).astype(BF16)[None]
            n0 = jnp.broadcast_to(n_ref[ii, head:head + 1, :], (BF16_ROWS, tb)).astype(BF16)[None]
            rows = slice(head * nk, (head + 1) * nk)
            e1 = e1_ref[rows, :].reshape(packed)
            r1 = r_ref[rows, :].reshape(packed)
            w = w + jnp.where(r1 < n0, e0 * e1, jnp.zeros((), BF16))
        x = pre[ii * nk:(ii + 1) * nk, :]
        act = 0.5 * x * (1.0 + lax.erf(x * (1.0 / math.sqrt(2.0))))
        pt_ref[slot, ii * nk:(ii + 1) * nk, :] = w.reshape(nk, tb) * act.astype(BF16)

    @pl.when(e == pl.num_programs(1) - 1)
    def _():
        h2 = h_ref[...] + acc_ref[...].T
        y_ref[...] = h2 * lax.rsqrt(jnp.mean(h2 * h2, axis=-1, keepdims=True) + EPS) * gf_ref[...]


def _experts(hn16, u16, vt16, e0, nsel, e1, rank1, h, final_g, tb, ec):
    t, d = h.shape
    nchunk = u16.shape[0] // ec
    this = lambda e: jnp.minimum(e, nchunk - 1)
    prev = lambda e: jnp.maximum(e - 1, 0)
    tok = pl.BlockSpec((tb, d), lambda i, e: (i, 0))
    by_key = pl.BlockSpec((ec // PEER_NKEYS, PEER_HEADS, tb), lambda i, e: (this(e), 0, i))
    by_head = pl.BlockSpec((e1.shape[0], tb), lambda i, e: (0, i))
    return pl.pallas_call(
        functools.partial(_experts_kernel, ec=ec),
        grid=(t // tb, nchunk + 1),
        in_specs=[tok, pl.BlockSpec((ec, d), lambda i, e: (this(e), 0)),
                  pl.BlockSpec((None, d, ec), lambda i, e: (prev(e), 0, 0)),
                  by_key, by_key, by_head, by_head,
                  pl.BlockSpec((tb, d), lambda i, e: (i, 0), pipeline_mode=pl.Buffered(1)),
                  pl.BlockSpec((1, d), lambda i, e: (0, 0))],
        out_specs=tok,
        out_shape=jax.ShapeDtypeStruct((t, d), F32),
        scratch_shapes=[pltpu.VMEM((d, tb), F32), pltpu.VMEM((2, ec, tb), BF16)],
        compiler_params=_params(("parallel", "arbitrary")),
        name="experts",
    )(hn16, u16, vt16, e0, nsel, e1, rank1, h, final_g)


def _block_rows(t, want):
    b = min(want, t)
    while t % b:
        b //= 2
    return b


SB_BLOCK = 256
EXPERT_CHUNK = 1024


def _group(x, state0, cache_k, cache_v, w, gla_chunks):
    nseq, tseq, d = x.shape
    t = nseq * tseq
    xf = x.reshape(t, d)
    qa, ka, va, ga, gk, qb16, kb, vb, kb16, vb16 = _in_proj(
        xf, w["norm1_g"], w["w_in16"], w["w_la"], w["w_gk2p"], w["b_gk"], _block_rows(t, 256))
    a16, state = _gla(qa, ka, va, gk, ga, w["gla_norm_g"], state0, gla_chunks)
    if cache_k is None:
        s16 = _sb(qb16, kb16, vb16, w["sb_norm_g"], nseq, SB_BLOCK)
    else:
        s16 = _sb_cached(qb16, kb16, vb16, cache_k, cache_v, w["sb_norm_g"], _block_rows(cache_k.shape[1], SB_BLOCK))
    h, hn16, scores_t = _mid(xf, a16, s16, w["w_out16"], w["norm2_g"], w["w_query16"], w["sub_keys16"],
                             _block_rows(t, 256))
    e0, nsel, e1, rank1 = _route(scores_t, LANES)
    y = _experts(hn16, w["u16"], w["vt16"], e0, nsel, e1, rank1, h, w["final_g"], _block_rows(t, 512), EXPERT_CHUNK)
    return (y.reshape(nseq, tseq, d), state,
            kb.reshape(nseq, tseq, SB_HEADS, SB_HD), vb.reshape(nseq, tseq, SB_HEADS, SB_HD))


def kernel(x_prompt, x_sample, cache_k, cache_v, state_gla, norm1_g, w_in, w_gk2, b_gk, gla_norm_g, sb_norm_g,
           w_out, norm2_g, w_query, sub_keys, expert_u, expert_v, final_g):
    d = x_prompt.shape[-1]
    la0 = 2 * GLA_QK + 2 * GLA_V
    w_la = jnp.pad(w_in[:, la0:la0 + GLA_LOWRANK], ((0, 0), (0, LANES - GLA_LOWRANK)))
    w_gk2p = jnp.pad(w_gk2, ((0, LANES - GLA_LOWRANK), (0, 0)))
    ne = expert_v.shape[0]
    vt16 = expert_v.reshape(ne // EXPERT_CHUNK, EXPERT_CHUNK, d).transpose(0, 2, 1).astype(BF16)
    w = dict(
        norm1_g=norm1_g.reshape(1, d), w_in16=w_in.astype(BF16), w_la=w_la, w_gk2p=w_gk2p, b_gk=b_gk.reshape(1, GLA_QK),
        gla_norm_g=gla_norm_g.reshape(1, GLA_DV), sb_norm_g=sb_norm_g.reshape(1, SB_HD),
        w_out16=w_out.astype(BF16), norm2_g=norm2_g.reshape(1, d), w_query16=w_query.astype(BF16),
        sub_keys16=sub_keys.astype(BF16), u16=expert_u.astype(BF16), vt16=vt16,
        final_g=final_g.reshape(1, d))

    nb = x_prompt.shape[0]
    zero_state = jnp.zeros((nb, GLA_HEADS, GLA_DK, GLA_DV), F32)
    y_p, state_p, k_p, v_p = _group(x_prompt, zero_state, None, None, w, gla_chunks=4)
    y_s, state_s, k_s, v_s = _group(x_sample, state_gla.astype(F32), cache_k, cache_v, w, gla_chunks=1)
    return (y_p, y_s, state_p, k_p, v_p, state_s, k_s, v_s)
```
